```python
import jax, jax.numpy as jnp
from jax import lax
import numpy as np

D_MODEL = 2048
BATCH = 2
SEQ = 4096
DEPTH = 1

GRID_W = 64
CTX_LEN = 256
MIX_WIDTH = D_MODEL
HEAD_DIM = 128
ATTN_WIDTH = MIX_WIDTH // 2
GMLP_WIDTH = MIX_WIDTH - ATTN_WIDTH
N_Q_HEADS = ATTN_WIDTH // HEAD_DIM
N_KV_HEADS = 2
KV_WIDTH = N_KV_HEADS * HEAD_DIM
GMLP_HEADS = 8
GMLP_HEAD_DIM = GMLP_WIDTH // GMLP_HEADS
CHUNK = 128
WINDOW = 128
BLOCK = 128
D_FF = 4 * D_MODEL
N_MOD = 6
Q_END = ATTN_WIDTH
K_END = Q_END + KV_WIDTH
V_END = K_END + KV_WIDTH
U_END = V_END + GMLP_WIDTH
IN_WIDTH = U_END + GMLP_WIDTH
ROPE_BASE = 10000.0
LN_EPS = 1e-5
NEG_INF = -1e30
ALPHA = (2 * DEPTH) ** 0.25
BETA = (8 * DEPTH) ** -0.25

kernel_name = "hymba_style_window_gqa_gmlp_dit_block"


def layer_norm(x, g, b):
    xf = x.astype(jnp.float32)
    mu = jnp.mean(xf, axis=-1, keepdims=True)
    var = jnp.mean(jnp.square(xf - mu), axis=-1, keepdims=True)
    y = (xf - mu) * lax.rsqrt(var + LN_EPS)
    return (y * g.astype(jnp.float32) + b.astype(jnp.float32)).astype(x.dtype)


def modulate(x, shift, scale):
    return x * (1 + scale) + shift


def adaln(cond, w_ada, b_ada):
    m = jax.nn.silu(cond) @ w_ada + b_ada
    return m.reshape(cond.shape[0], N_MOD, D_MODEL)


def axial_rope_tables(n_tokens):
    rows = n_tokens // GRID_W
    row = jnp.repeat(jnp.arange(rows, dtype=jnp.float32), GRID_W)
    col = jnp.tile(jnp.arange(GRID_W, dtype=jnp.float32), rows)
    n_freq = HEAD_DIM // 4
    inv_freq = ROPE_BASE ** (-jnp.arange(n_freq, dtype=jnp.float32) / n_freq)
    ang_r = row[:, None] * inv_freq[None, :]
    ang_c = col[:, None] * inv_freq[None, :]
    return (jnp.cos(ang_r), jnp.sin(ang_r), jnp.cos(ang_c), jnp.sin(ang_c))


def rotate_half_rope(x, cos, sin):
    x1, x2 = jnp.split(x, 2, axis=-1)
    cos = cos[None, :, None, :]
    sin = sin[None, :, None, :]
    return jnp.concatenate([x1 * cos - x2 * sin, x1 * sin + x2 * cos], axis=-1)


def axial_rope(x, tables):
    cos_r, sin_r, cos_c, sin_c = tables
    xr, xc = jnp.split(x, 2, axis=-1)
    y = jnp.concatenate([rotate_half_rope(xr, cos_r, sin_r),
                         rotate_half_rope(xc, cos_c, sin_c)], axis=-1)
    return y.astype(x.dtype)


def split_projection(proj):
    b, s, _ = proj.shape
    q, k, v, u, g = jnp.split(proj, [Q_END, K_END, V_END, U_END], axis=-1)
    q = q.reshape(b, s, N_Q_HEADS, HEAD_DIM)
    k = k.reshape(b, s, N_KV_HEADS, HEAD_DIM)
    v = v.reshape(b, s, N_KV_HEADS, HEAD_DIM)
    return q, k, v, u, g


def windowed_attention(q, k, v, k_ctx, v_ctx, sink):
    b, s, hq, d = q.shape
    nb = s // BLOCK
    grp = hq // N_KV_HEADS
    scale = d ** -0.5
    qb = q.reshape(b, nb, BLOCK, N_KV_HEADS, grp, d)
    pad = ((0, 0), (BLOCK, BLOCK), (0, 0), (0, 0))
    kp = jnp.pad(k, pad).reshape(b, nb + 2, BLOCK, N_KV_HEADS, d)
    vp = jnp.pad(v, pad).reshape(b, nb + 2, BLOCK, N_KV_HEADS, d)
    kb = jnp.concatenate([kp[:, :-2], kp[:, 1:-1], kp[:, 2:]], axis=2)
    vb = jnp.concatenate([vp[:, :-2], vp[:, 1:-1], vp[:, 2:]], axis=2)
    s_loc = jnp.einsum('bnqhgd,bnkhd->bnhgqk', qb, kb,
                       preferred_element_type=jnp.float32) * scale
    q_idx = jnp.arange(nb)[:, None, None] * BLOCK + jnp.arange(BLOCK)[None, :, None]
    k_idx = jnp.arange(nb)[:, None, None] * BLOCK - BLOCK + jnp.arange(3 * BLOCK)[None, None, :]
    band = (jnp.abs(k_idx - q_idx) <= WINDOW) & (k_idx >= 0) & (k_idx < s)
    s_loc = jnp.where(band[None, :, None, None], s_loc, NEG_INF)
    s_ctx = jnp.einsum('bnqhgd,bchd->bnhgqc', qb, k_ctx,
                       preferred_element_type=jnp.float32) * scale
    s_sink = jnp.broadcast_to(
        sink.astype(jnp.float32).reshape(1, 1, N_KV_HEADS, grp, 1, 1),
        s_loc.shape[:-1] + (1,))
    p = jax.nn.softmax(jnp.concatenate([s_loc, s_ctx, s_sink], axis=-1), axis=-1)
    n_loc = 3 * BLOCK
    n_ctx = k_ctx.shape[1]
    p_loc = p[..., :n_loc].astype(v.dtype)
    p_ctx = p[..., n_loc:n_loc + n_ctx].astype(v.dtype)
    o = (jnp.einsum('bnhgqk,bnkhd->bnqhgd', p_loc, vb)
         + jnp.einsum('bnhgqc,bchd->bnqhgd', p_ctx, v_ctx))
    return o.reshape(b, s, hq * d)


def context_attention(q, k, v, sink):
    b, n, hq, d = q.shape
    grp = hq // N_KV_HEADS
    qg = q.reshape(b, n, N_KV_HEADS, grp, d)
    sc = jnp.einsum('bqhgd,bkhd->bhgqk', qg, k,
                    preferred_element_type=jnp.float32) * (d ** -0.5)
    s_sink = jnp.broadcast_to(
        sink.astype(jnp.float32).reshape(1, N_KV_HEADS, grp, 1, 1), sc.shape[:-1] + (1,))
    p = jax.nn.softmax(jnp.concatenate([sc, s_sink], axis=-1), axis=-1)
    o = jnp.einsum('bhgqk,bkhd->bqhgd', p[..., :n].astype(v.dtype), v)
    return o.reshape(b, n, hq * d)


def chunk_spatial_gate(u, g, ln_g, ln_b, w_s, b_s):
    b, s, _ = g.shape
    u = jax.nn.gelu(u)
    g = layer_norm(jax.nn.gelu(g), ln_g, ln_b)
    gc = g.reshape(b, s // CHUNK, CHUNK, GMLP_HEADS, GMLP_HEAD_DIM)
    mixed = jnp.einsum('hpq,bnqhd->bnphd', w_s, gc) + b_s.T[None, None, :, :, None]
    return u * mixed.reshape(b, s, GMLP_WIDTH)


def squared_relu_mlp(h, w1, w2):
    return jnp.square(jax.nn.relu(h @ w1)) @ w2


def context_kv(h_ctx, w_in):
    b, n, _ = h_ctx.shape
    kv = h_ctx @ w_in[:, Q_END:V_END]
    k, v = jnp.split(kv, 2, axis=-1)
    return (k.reshape(b, n, N_KV_HEADS, HEAD_DIM), v.reshape(b, n, N_KV_HEADS, HEAD_DIM))


def context_mixer(h_ctx, w_in, sink, gln_g, gln_b, w_s, b_s, w_out):
    q, k, v, u, g = split_projection(h_ctx @ w_in)
    attn = context_attention(q, k, v, sink)
    gm = chunk_spatial_gate(u, g, gln_g, gln_b, w_s, b_s)
    return jnp.concatenate([attn, gm], axis=-1) @ w_out, k, v


def latent_mixer(h, k_ctx, v_ctx, rope, w_in, sink, gln_g, gln_b, w_s, b_s, w_out):
    q, k, v, u, g = split_projection(h @ w_in)
    q = axial_rope(q, rope)
    k = axial_rope(k, rope)
    attn = windowed_attention(q, k, v, k_ctx, v_ctx, sink)
    gm = chunk_spatial_gate(u, g, gln_g, gln_b, w_s, b_s)
    return jnp.concatenate([attn, gm], axis=-1) @ w_out


def setup_inputs(seed: int = 0) -> dict:
    key = jax.random.key(seed)
    ks = jax.random.split(key, 20)
    f32 = jnp.float32
    nrm = lambda k, shape, s: jax.random.normal(k, shape, f32) * s
    return {
        "x": nrm(ks[0], (BATCH, SEQ, D_MODEL), 1.0),
        "c": nrm(ks[1], (BATCH, D_MODEL), 1.0),
        "ctx": nrm(ks[2], (BATCH, CTX_LEN, D_MODEL), 1.0),
        "c_ctx": nrm(ks[3], (D_MODEL,), 1.0),
        "w_ada": nrm(ks[4], (DEPTH, D_MODEL, N_MOD * D_MODEL), 0.5 * D_MODEL ** -0.5),
        "b_ada": nrm(ks[5], (DEPTH, N_MOD * D_MODEL), 0.02),
        "w_in": nrm(ks[6], (DEPTH, D_MODEL, IN_WIDTH), D_MODEL ** -0.5),
        "attn_sink": nrm(ks[7], (DEPTH, N_Q_HEADS), 1.0),
        "gmlp_ln_g": 1.0 + nrm(ks[8], (DEPTH, GMLP_WIDTH), 0.02),
        "gmlp_ln_b": nrm(ks[9], (DEPTH, GMLP_WIDTH), 0.02),
        "gmlp_w_s": nrm(ks[10], (DEPTH, GMLP_HEADS, CHUNK, CHUNK), CHUNK ** -0.5),
        "gmlp_b_s": 1.0 + nrm(ks[11], (DEPTH, GMLP_HEADS, CHUNK), 0.02),
        "w_out": nrm(ks[12], (DEPTH, MIX_WIDTH, D_MODEL), BETA * MIX_WIDTH ** -0.5),
        "ln1_g": 1.0 + nrm(ks[13], (DEPTH, D_MODEL), 0.02),
        "ln1_b": nrm(ks[14], (DEPTH, D_MODEL), 0.02),
        "w_ff1": nrm(ks[15], (DEPTH, D_MODEL, D_FF), D_MODEL ** -0.5),
        "w_ff2": nrm(ks[16], (DEPTH, D_FF, D_MODEL), BETA * D_FF ** -0.5),
        "ln2_g": 1.0 + nrm(ks[17], (DEPTH, D_MODEL), 0.02),
        "ln2_b": nrm(ks[18], (DEPTH, D_MODEL), 0.02),
    }


def reference(x, c, ctx, c_ctx, w_ada, b_ada, w_in, attn_sink, gmlp_ln_g, gmlp_ln_b,
              gmlp_w_s, gmlp_b_s, w_out, ln1_g, ln1_b, w_ff1, w_ff2, ln2_g, ln2_b):
    rope = axial_rope_tables(x.shape[1])
    for l in range(DEPTH):
        mod = adaln(c, w_ada[l], b_ada[l])[:, :, None, :]
        mod_c = adaln(c_ctx[None, :], w_ada[l], b_ada[l])[:, :, None, :]
        mixer_params = (w_in[l], attn_sink[l], gmlp_ln_g[l], gmlp_ln_b[l],
                        gmlp_w_s[l], gmlp_b_s[l], w_out[l])
        h_ctx = modulate(ctx, mod_c[:, 0], mod_c[:, 1])
        if l < DEPTH - 1:
            mix_ctx, k_ctx, v_ctx = context_mixer(h_ctx, *mixer_params)
            ctx_new = layer_norm(ALPHA * ctx + mod_c[:, 2] * mix_ctx, ln1_g[l], ln1_b[l])
            ff_ctx = squared_relu_mlp(modulate(ctx_new, mod_c[:, 3], mod_c[:, 4]),
                                      w_ff1[l], w_ff2[l])
            ctx_next = layer_norm(ALPHA * ctx_new + mod_c[:, 5] * ff_ctx, ln2_g[l], ln2_b[l])
        else:
            k_ctx, v_ctx = context_kv(h_ctx, w_in[l])
            ctx_next = ctx
        h = modulate(x, mod[:, 0], mod[:, 1])
        mix = latent_mixer(h, k_ctx, v_ctx, rope, *mixer_params)
        x = layer_norm(ALPHA * x + mod[:, 2] * mix, ln1_g[l], ln1_b[l])
        ff = squared_relu_mlp(modulate(x, mod[:, 3], mod[:, 4]), w_ff1[l], w_ff2[l])
        x = layer_norm(ALPHA * x + mod[:, 5] * ff, ln2_g[l], ln2_b[l])
        ctx = ctx_next
    return x
```

```python
import functools

import jax
import jax.numpy as jnp
from jax import lax
from jax.experimental import pallas as pl
from jax.experimental.pallas import tpu as pltpu

BF16 = jnp.bfloat16
F32 = jnp.float32

GRID_W = 64
HEAD_DIM = 128
N_Q_HEADS = 8
N_KV_HEADS = 2
GROUP = N_Q_HEADS // N_KV_HEADS
GMLP_HEADS = 8
CHUNK = 128
WINDOW = 128
BLOCK = 128
N_MOD = 6
ROPE_BASE = 10000.0
LN_EPS = 1e-5
NEG_INF = -1e30

VMEM_LIMIT_BYTES = 56 * 1024 * 1024


def _params(n_axes):
    return pltpu.CompilerParams(
        dimension_semantics=("arbitrary",) * n_axes, vmem_limit_bytes=VMEM_LIMIT_BYTES)


def _layer_norm(y, g, b):
    mu = jnp.mean(y, axis=-1, keepdims=True)
    d = y - mu
    var = jnp.mean(d * d, axis=-1, keepdims=True)
    return d * lax.rsqrt(var + LN_EPS) * g + b


def _adaln_kernel(cond_ref, w_ref, b_ref, o_ref):
    c = cond_ref[...]
    s = (c * jax.nn.sigmoid(c)).astype(BF16)
    o_ref[...] = jnp.dot(s, w_ref[...].astype(BF16), preferred_element_type=F32) + b_ref[...]


def _adaln(cond, w_ada, b_ada, tn=1024):
    rows, d = cond.shape
    n = w_ada.shape[1]
    return pl.pallas_call(
        _adaln_kernel,
        grid=(n // tn,),
        in_specs=[pl.BlockSpec((rows, d), lambda j: (0, 0)),
                  pl.BlockSpec((d, tn), lambda j: (0, j)),
                  pl.BlockSpec((1, tn), lambda j: (0, j))],
        out_specs=pl.BlockSpec((rows, tn), lambda j: (0, j)),
        out_shape=jax.ShapeDtypeStruct((rows, n), F32),
        compiler_params=_params(1),
        name="adaln",
    )(cond, w_ada, b_ada)


def _ctx_kv_kernel(ctx_ref, shift_ref, scale_ref, w_ref, k_ref, v_ref):
    h = (ctx_ref[...] * (1.0 + scale_ref[0]) + shift_ref[0]).astype(BF16)
    kv = jnp.dot(h, w_ref[...], preferred_element_type=F32)
    kvw = k_ref.shape[1]
    k_ref[...] = kv[:, :kvw].astype(BF16)
    v_ref[...] = kv[:, kvw:].astype(BF16)


def _ctx_kv(ctx2, mod, w_in_bf, ctx_row, n_ctx, q_width, kv_width):
    rows, d = ctx2.shape
    assert q_width % (2 * kv_width) == 0
    out = jax.ShapeDtypeStruct((rows, kv_width), BF16)
    return pl.pallas_call(
        _ctx_kv_kernel,
        grid=(rows // n_ctx,),
        in_specs=[pl.BlockSpec((n_ctx, d), lambda i: (i, 0)),
                  pl.BlockSpec((1, 1, d), lambda i: (ctx_row * N_MOD + 0, 0, 0)),
                  pl.BlockSpec((1, 1, d), lambda i: (ctx_row * N_MOD + 1, 0, 0)),
                  pl.BlockSpec((d, 2 * kv_width), lambda i: (0, q_width // (2 * kv_width)))],
        out_specs=[pl.BlockSpec((n_ctx, kv_width), lambda i: (i, 0)),
                   pl.BlockSpec((n_ctx, kv_width), lambda i: (i, 0))],
        out_shape=[out, out],
        compiler_params=_params(1),
        name="ctx_kv",
    )(ctx2, mod, mod, w_in_bf)


def _in_proj_kernel(x_ref, shift_ref, scale_ref, w_ref, cos_ref, sin_ref, lng_ref, lnb_ref,
                    ws_ref, bs_ref, q_ref, k_ref, v_ref, gm_ref):
    tm = x_ref.shape[0]
    qw = q_ref.shape[1]
    kvw = k_ref.shape[1]
    gw = gm_ref.shape[1]
    h = (x_ref[...] * (1.0 + scale_ref[0]) + shift_ref[0]).astype(BF16)

    cos = cos_ref[...]
    sin = sin_ref[...]
    lane = lax.broadcasted_iota(jnp.int32, (tm, HEAD_DIM), 1)
    low = (lane % (HEAD_DIM // 2)) < (HEAD_DIM // 4)

    def rope(t):
        partner = jnp.where(low, pltpu.roll(t, HEAD_DIM - HEAD_DIM // 4, 1),
                            pltpu.roll(t, HEAD_DIM // 4, 1))
        return t * cos + partner * sin

    def proj(lo, width):
        return jnp.dot(h, w_ref[:, lo:lo + width], preferred_element_type=F32)

    qk = proj(0, qw + kvw)
    for hd in range((qw + kvw) // HEAD_DIM):
        t = rope(qk[:, hd * HEAD_DIM:(hd + 1) * HEAD_DIM]).astype(BF16)
        if hd * HEAD_DIM < qw:
            q_ref[:, hd * HEAD_DIM:(hd + 1) * HEAD_DIM] = t
        else:
            k_ref[:, hd * HEAD_DIM - qw:(hd + 1) * HEAD_DIM - qw] = t
    v_ref[...] = proj(qw + kvw, kvw).astype(BF16)

    u = jax.nn.gelu(proj(qw + 2 * kvw, gw))
    g = jax.nn.gelu(proj(qw + 2 * kvw + gw, gw))
    g = _layer_norm(g, lng_ref[...], lnb_ref[...]).astype(BF16)
    hdim = gw // GMLP_HEADS
    for cb in range(tm // CHUNK):
        rows = slice(cb * CHUNK, (cb + 1) * CHUNK)
        for hd in range(GMLP_HEADS):
            cols = slice(hd * hdim, (hd + 1) * hdim)
            mixed = jnp.dot(ws_ref[hd], g[rows, cols], preferred_element_type=F32) + bs_ref[hd]
            gm_ref[rows, cols] = (u[rows, cols] * mixed).astype(BF16)


def _in_proj(x2, mod, w_in_bf, cos, sin, ln_g, ln_b, ws_bf, bs_b, seq, q_width, kv_width, g_width,
             tm=512):
    n, d = x2.shape
    tiles_per_seq = seq // tm
    hdim = g_width // GMLP_HEADS
    const2 = lambda i: (0, 0)
    const3 = lambda i: (0, 0, 0)
    return pl.pallas_call(
        _in_proj_kernel,
        grid=(n // tm,),
        in_specs=[pl.BlockSpec((tm, d), lambda i: (i, 0)),
                  pl.BlockSpec((1, 1, d), lambda i: ((i // tiles_per_seq) * N_MOD + 0, 0, 0)),
                  pl.BlockSpec((1, 1, d), lambda i: ((i // tiles_per_seq) * N_MOD + 1, 0, 0)),
                  pl.BlockSpec(w_in_bf.shape, const2, pipeline_mode=pl.Buffered(1)),
                  pl.BlockSpec((tm, HEAD_DIM), lambda i: (i % tiles_per_seq, 0)),
                  pl.BlockSpec((tm, HEAD_DIM), lambda i: (i % tiles_per_seq, 0)),
                  pl.BlockSpec((1, g_width), const2),
                  pl.BlockSpec((1, g_width), const2),
                  pl.BlockSpec((GMLP_HEADS, CHUNK, CHUNK), const3),
                  pl.BlockSpec((GMLP_HEADS, CHUNK, hdim), const3)],
        out_specs=[pl.BlockSpec((tm, q_width), lambda i: (i, 0)),
                   pl.BlockSpec((tm, kv_width), lambda i: (i, 0)),
                   pl.BlockSpec((tm, kv_width), lambda i: (i, 0)),
                   pl.BlockSpec((tm, g_width), lambda i: (i, 0))],
        out_shape=[jax.ShapeDtypeStruct((n, q_width), BF16),
                   jax.ShapeDtypeStruct((n, kv_width), BF16),
                   jax.ShapeDtypeStruct((n, kv_width), BF16),
                   jax.ShapeDtypeStruct((n, g_width), BF16)],
        compiler_params=_params(1),
        name="in_proj",
    )(x2, mod, mod, w_in_bf, cos, sin, ln_g, ln_b, ws_bf, bs_b)


def _attn_kernel(sink_ref, q_ref, k_ref, v_ref, kc_ref, vc_ref, o_ref):
    tq = q_ref.shape[0]
    seq = k_ref.shape[0]
    n_loc = 3 * BLOCK
    rows = GROUP * BLOCK
    scale = HEAD_DIM ** -0.5
    t = pl.program_id(1)
    nt = (((1,), (1,)), ((), ()))

    row = lax.broadcasted_iota(jnp.int32, (rows, n_loc), 0)
    col = lax.broadcasted_iota(jnp.int32, (rows, n_loc), 1)
    rel = col - row % BLOCK
    head_of_row = lax.broadcasted_iota(jnp.int32, (rows, 1), 0) // BLOCK

    for hk in range(N_KV_HEADS):
        kv_cols = slice(hk * HEAD_DIM, (hk + 1) * HEAD_DIM)
        kc = kc_ref[:, kv_cols]
        vc = vc_ref[:, kv_cols]
        sink = jnp.zeros((rows, 1), F32)
        for g in range(GROUP):
            sink = jnp.where(head_of_row == g, sink_ref[hk * GROUP + g], sink)
        for jq in range(tq // BLOCK):
            blk = t * (tq // BLOCK) + jq
            ks = pl.multiple_of(jnp.clip((blk - 1) * BLOCK, 0, seq - n_loc), BLOCK)
            kl = k_ref[pl.ds(ks, n_loc), kv_cols]
            vl = v_ref[pl.ds(ks, n_loc), kv_cols]
            qs = jnp.concatenate(
                [q_ref[jq * BLOCK:(jq + 1) * BLOCK,
                       (hk * GROUP + g) * HEAD_DIM:(hk * GROUP + g + 1) * HEAD_DIM]
                 for g in range(GROUP)], axis=0)
            s_loc = lax.dot_general(qs, kl, nt, preferred_element_type=F32) * scale
            s_ctx = lax.dot_general(qs, kc, nt, preferred_element_type=F32) * scale
            band = jnp.abs(rel + (ks - blk * BLOCK)) <= WINDOW
            s_loc = jnp.where(band, s_loc, NEG_INF)
            m = jnp.maximum(jnp.maximum(jnp.max(s_loc, axis=-1, keepdims=True),
                                        jnp.max(s_ctx, axis=-1, keepdims=True)), sink)
            p_loc = jnp.exp(s_loc - m)
            p_ctx = jnp.exp(s_ctx - m)
            denom = (jnp.sum(p_loc, axis=-1, keepdims=True) + jnp.sum(p_ctx, axis=-1, keepdims=True)
                     + jnp.exp(sink - m))
            o = (jnp.dot(p_loc.astype(BF16), vl, preferred_element_type=F32)
                 + jnp.dot(p_ctx.astype(BF16), vc, preferred_element_type=F32)) / denom
            for g in range(GROUP):
                o_ref[jq * BLOCK:(jq + 1) * BLOCK,
                      (hk * GROUP + g) * HEAD_DIM:(hk * GROUP + g + 1) * HEAD_DIM] = (
                          o[g * BLOCK:(g + 1) * BLOCK].astype(BF16))


def _attention(sink, q, k, v, kc, vc, batch, seq, n_ctx, tq=512):
    q_width = q.shape[1]
    kv_width = k.shape[1]
    return pl.pallas_call(
        _attn_kernel,
        grid=(batch, seq // tq),
        in_specs=[pl.BlockSpec(memory_space=pltpu.SMEM),
                  pl.BlockSpec((tq, q_width), lambda b, t: (b * (seq // tq) + t, 0)),
                  pl.BlockSpec((seq, kv_width), lambda b, t: (b, 0)),
                  pl.BlockSpec((seq, kv_width), lambda b, t: (b, 0)),
                  pl.BlockSpec((n_ctx, kv_width), lambda b, t: (b, 0)),
                  pl.BlockSpec((n_ctx, kv_width), lambda b, t: (b, 0))],
        out_specs=pl.BlockSpec((tq, q_width), lambda b, t: (b * (seq // tq) + t, 0)),
        out_shape=jax.ShapeDtypeStruct(q.shape, BF16),
        compiler_params=_params(2),
        name="attention",
    )(sink, q, k, v, kc, vc)


def _out_proj_kernel(alpha, attn_ref, gm_ref, x_ref, gate_ref, shift_ref, scale_ref, w_ref,
                     g_ref, b_ref, x1_ref, h2_ref):
    aw = attn_ref.shape[1]
    mix = (jnp.dot(attn_ref[...], w_ref[:aw, :], preferred_element_type=F32)
           + jnp.dot(gm_ref[...], w_ref[aw:, :], preferred_element_type=F32))
    x1 = _layer_norm(alpha * x_ref[...] + gate_ref[0] * mix, g_ref[...], b_ref[...])
    x1_ref[...] = x1
    h2_ref[...] = (x1 * (1.0 + scale_ref[0]) + shift_ref[0]).astype(BF16)


def _out_proj(attn, gm, x2, mod, w_out_bf, ln_g, ln_b, seq, alpha, tm=512):
    n, d = x2.shape
    tiles_per_seq = seq // tm
    const2 = lambda i: (0, 0)
    mod_spec = lambda j: pl.BlockSpec((1, 1, d), lambda i: ((i // tiles_per_seq) * N_MOD + j, 0, 0))
    return pl.pallas_call(
        functools.partial(_out_proj_kernel, alpha),
        grid=(n // tm,),
        in_specs=[pl.BlockSpec((tm, attn.shape[1]), lambda i: (i, 0)),
                  pl.BlockSpec((tm, gm.shape[1]), lambda i: (i, 0)),
                  pl.BlockSpec((tm, d), lambda i: (i, 0)),
                  mod_spec(2), mod_spec(3), mod_spec(4),
                  pl.BlockSpec(w_out_bf.shape, const2, pipeline_mode=pl.Buffered(1)),
                  pl.BlockSpec((1, d), const2),
                  pl.BlockSpec((1, d), const2)],
        out_specs=[pl.BlockSpec((tm, d), lambda i: (i, 0)),
                   pl.BlockSpec((tm, d), lambda i: (i, 0))],
        out_shape=[jax.ShapeDtypeStruct((n, d), F32),
                   jax.ShapeDtypeStruct((n, d), BF16)],
        compiler_params=_params(1),
        name="out_proj",
    )(attn, gm, x2, mod, mod, mod, w_out_bf, ln_g, ln_b)


def _ffn_kernel(alpha, h_ref, x1_ref, gate_ref, w1_ref, w2_ref, g_ref, b_ref, o_ref, acc_ref):
    j = pl.program_id(1)

    @pl.when(j == 0)
    def _():
        acc_ref[...] = jnp.zeros_like(acc_ref)

    a = jnp.maximum(jnp.dot(h_ref[...], w1_ref[...], preferred_element_type=F32), 0.0)
    acc_ref[...] += jnp.dot((a * a).astype(BF16), w2_ref[...], preferred_element_type=F32)

    @pl.when(j == pl.num_programs(1) - 1)
    def _():
        o_ref[...] = _layer_norm(alpha * x1_ref[...] + gate_ref[0] * acc_ref[...],
                                 g_ref[...], b_ref[...])


def _ffn(h2, x1, mod, w1_bf, w2_bf, ln_g, ln_b, seq, alpha, tm=512, tf=1024):
    n, d = x1.shape
    d_ff = w1_bf.shape[1]
    tiles_per_seq = seq // tm
    const2 = lambda i, j: (0, 0)
    return pl.pallas_call(
        functools.partial(_ffn_kernel, alpha),
        grid=(n // tm, d_ff // tf),
        in_specs=[pl.BlockSpec((tm, d), lambda i, j: (i, 0)),
                  pl.BlockSpec((tm, d), lambda i, j: (i, 0)),
                  pl.BlockSpec((1, 1, d), lambda i, j: ((i // tiles_per_seq) * N_MOD + 5, 0, 0)),
                  pl.BlockSpec((d, tf), lambda i, j: (0, j)),
                  pl.BlockSpec((tf, d), lambda i, j: (j, 0)),
                  pl.BlockSpec((1, d), const2),
                  pl.BlockSpec((1, d), const2)],
        out_specs=pl.BlockSpec((tm, d), lambda i, j: (i, 0)),
        out_shape=jax.ShapeDtypeStruct((n, d), F32),
        scratch_shapes=[pltpu.VMEM((tm, d), F32)],
        compiler_params=_params(2),
        name="ffn",
    )(h2, x1, mod, w1_bf, w2_bf, ln_g, ln_b)


def _rope_tables(seq):
    rows = seq // GRID_W
    row = jnp.repeat(jnp.arange(rows, dtype=F32), GRID_W)
    col = jnp.tile(jnp.arange(GRID_W, dtype=F32), rows)
    n_freq = HEAD_DIM // 4
    inv_freq = ROPE_BASE ** (-jnp.arange(n_freq, dtype=F32) / n_freq)
    ang_r = row[:, None] * inv_freq[None, :]
    ang_c = col[:, None] * inv_freq[None, :]
    cos = jnp.concatenate([jnp.cos(ang_r), jnp.cos(ang_r), jnp.cos(ang_c), jnp.cos(ang_c)], axis=-1)
    sin = jnp.concatenate([-jnp.sin(ang_r), jnp.sin(ang_r), -jnp.sin(ang_c), jnp.sin(ang_c)], axis=-1)
    return cos, sin


def kernel(x, c, ctx, c_ctx, w_ada, b_ada, w_in, attn_sink, gmlp_ln_g, gmlp_ln_b, gmlp_w_s, gmlp_b_s,
           w_out, ln1_g, ln1_b, w_ff1, w_ff2, ln2_g, ln2_b):
    batch, seq, d = x.shape
    depth = w_ada.shape[0]
    assert depth == 1, "only the single-layer (last-layer) block is implemented"
    n_ctx = ctx.shape[1]
    q_width = N_Q_HEADS * HEAD_DIM
    kv_width = N_KV_HEADS * HEAD_DIM
    g_width = gmlp_ln_g.shape[1]
    assert w_in.shape[2] == q_width + 2 * kv_width + 2 * g_width
    alpha = (2 * depth) ** 0.25

    cond_rows = 8
    cond = jnp.zeros((cond_rows, d), F32).at[:batch].set(c).at[batch].set(c_ctx)
    mod = _adaln(cond, w_ada[0], b_ada[0][None, :]).reshape(cond_rows * N_MOD, 1, d)

    w_in_bf = w_in[0].astype(BF16)
    w_out_bf = w_out[0].astype(BF16)
    w1_bf = w_ff1[0].astype(BF16)
    w2_bf = w_ff2[0].astype(BF16)
    ws_bf = gmlp_w_s[0].astype(BF16)
    hdim = g_width // GMLP_HEADS
    bs_b = jnp.broadcast_to(gmlp_b_s[0][:, :, None], (GMLP_HEADS, CHUNK, hdim))

    x2 = x.reshape(batch * seq, d)
    kc, vc = _ctx_kv(ctx.reshape(batch * n_ctx, d), mod, w_in_bf, batch, n_ctx, q_width, kv_width)
    cos, sin = _rope_tables(seq)
    q, k, v, gm = _in_proj(x2, mod, w_in_bf, cos, sin, gmlp_ln_g[0][None, :], gmlp_ln_b[0][None, :],
                           ws_bf, bs_b, seq, q_width, kv_width, g_width)
    attn = _attention(attn_sink[0], q, k, v, kc, vc, batch, seq, n_ctx)
    x1, h2 = _out_proj(attn, gm, x2, mod, w_out_bf, ln1_g[0][None, :], ln1_b[0][None, :], seq, alpha)
    out = _ffn(h2, x1, mod, w1_bf, w2_bf, ln2_g[0][None, :], ln2_b[0][None, :], seq, alpha)
    return out.reshape(batch, seq, d)
```

```python
import functools

import jax
import jax.numpy as jnp
from jax import lax
from jax.experimental import pallas as pl
from jax.experimental.pallas import tpu as pltpu

BF16 = jnp.bfloat16
F32 = jnp.float32

GRID_W = 64
HEAD_DIM = 128
N_Q_HEADS = 8
N_KV_HEADS = 2
GROUP = N_Q_HEADS // N_KV_HEADS
GMLP_HEADS = 8
CHUNK = 128
WINDOW = 128
BLOCK = 128
N_MOD = 6
ROPE_BASE = 10000.0
LN_EPS = 1e-5
NEG_INF = -1e30

VMEM_LIMIT_BYTES = 56 * 1024 * 1024


def _params(n_axes):
    return pltpu.CompilerParams(
        dimension_semantics=("arbitrary",) * n_axes, vmem_limit_bytes=VMEM_LIMIT_BYTES)


def _side_rows(w, n_steps):
    rows = w.shape[0] // n_steps
    assert rows * n_steps == w.shape[0] and rows % 16 == 0
    return rows


def _layer_norm(y, g, b):
    mu = jnp.mean(y, axis=-1, keepdims=True)
    d = y - mu
    var = jnp.mean(d * d, axis=-1, keepdims=True)
    return d * lax.rsqrt(var + LN_EPS) * g + b


def _adaln_kernel(cond_ref, w_ref, b_ref, o_ref):
    c = cond_ref[...]
    s = (c * jax.nn.sigmoid(c)).astype(BF16)
    o_ref[...] = jnp.dot(s, w_ref[...].astype(BF16), preferred_element_type=F32) + b_ref[...]


def _adaln(cond, w_ada, b_ada, tn=1024):
    rows, d = cond.shape
    n = w_ada.shape[1]
    return pl.pallas_call(
        _adaln_kernel,
        grid=(n // tn,),
        in_specs=[pl.BlockSpec((rows, d), lambda j: (0, 0)),
                  pl.BlockSpec((d, tn), lambda j: (0, j)),
                  pl.BlockSpec((1, tn), lambda j: (0, j))],
        out_specs=pl.BlockSpec((rows, tn), lambda j: (0, j)),
        out_shape=jax.ShapeDtypeStruct((rows, n), F32),
        compiler_params=_params(1),
        name="adaln",
    )(cond, w_ada, b_ada)


def _ctx_kv_kernel(ctx_ref, shift_ref, scale_ref, w_ref, k_ref, v_ref):
    h = (ctx_ref[...] * (1.0 + scale_ref[0]) + shift_ref[0]).astype(BF16)
    kv = jnp.dot(h, w_ref[...], preferred_element_type=F32)
    kvw = k_ref.shape[1]
    k_ref[...] = kv[:, :kvw].astype(BF16)
    v_ref[...] = kv[:, kvw:].astype(BF16)


def _ctx_kv(ctx2, mod, w_in_bf, ctx_row, n_ctx, q_width, kv_width):
    rows, d = ctx2.shape
    assert q_width % (2 * kv_width) == 0
    out = jax.ShapeDtypeStruct((rows, kv_width), BF16)
    return pl.pallas_call(
        _ctx_kv_kernel,
        grid=(rows // n_ctx,),
        in_specs=[pl.BlockSpec((n_ctx, d), lambda i: (i, 0)),
                  pl.BlockSpec((1, 1, d), lambda i: (ctx_row * N_MOD + 0, 0, 0)),
                  pl.BlockSpec((1, 1, d), lambda i: (ctx_row * N_MOD + 1, 0, 0)),
                  pl.BlockSpec((d, 2 * kv_width), lambda i: (0, q_width // (2 * kv_width)))],
        out_specs=[pl.BlockSpec((n_ctx, kv_width), lambda i: (i, 0)),
                   pl.BlockSpec((n_ctx, kv_width), lambda i: (i, 0))],
        out_shape=[out, out],
        compiler_params=_params(1),
        name="ctx_kv",
    )(ctx2, mod, mod, w_in_bf)


def _in_proj_kernel(x_ref, shift_ref, scale_ref, w_ref, cos_ref, sin_ref, lng_ref, lnb_ref,
                    ws_ref, bs_ref, wside_ref, q_ref, k_ref, v_ref, gm_ref, wside_bf_ref):
    wside_bf_ref[...] = wside_ref[...].astype(BF16)
    tm = x_ref.shape[0]
    qw = q_ref.shape[1]
    kvw = k_ref.shape[1]
    gw = gm_ref.shape[1]
    h = (x_ref[...] * (1.0 + scale_ref[0]) + shift_ref[0]).astype(BF16)

    cos = cos_ref[...]
    sin = sin_ref[...]
    lane = lax.broadcasted_iota(jnp.int32, (tm, HEAD_DIM), 1)
    low = (lane % (HEAD_DIM // 2)) < (HEAD_DIM // 4)

    def rope(t):
        partner = jnp.where(low, pltpu.roll(t, HEAD_DIM - HEAD_DIM // 4, 1),
                            pltpu.roll(t, HEAD_DIM // 4, 1))
        return t * cos + partner * sin

    def proj(lo, width):
        return jnp.dot(h, w_ref[:, lo:lo + width], preferred_element_type=F32)

    qk = proj(0, qw + kvw)
    for hd in range((qw + kvw) // HEAD_DIM):
        t = rope(qk[:, hd * HEAD_DIM:(hd + 1) * HEAD_DIM]).astype(BF16)
        if hd * HEAD_DIM < qw:
            q_ref[:, hd * HEAD_DIM:(hd + 1) * HEAD_DIM] = t
        else:
            k_ref[:, hd * HEAD_DIM - qw:(hd + 1) * HEAD_DIM - qw] = t
    v_ref[...] = proj(qw + kvw, kvw).astype(BF16)

    u = jax.nn.gelu(proj(qw + 2 * kvw, gw))
    g = jax.nn.gelu(proj(qw + 2 * kvw + gw, gw))
    g = _layer_norm(g, lng_ref[...], lnb_ref[...]).astype(BF16)
    hdim = gw // GMLP_HEADS
    for cb in range(tm // CHUNK):
        rows = slice(cb * CHUNK, (cb + 1) * CHUNK)
        for hd in range(GMLP_HEADS):
            cols = slice(hd * hdim, (hd + 1) * hdim)
            mixed = jnp.dot(ws_ref[hd], g[rows, cols], preferred_element_type=F32) + bs_ref[hd]
            gm_ref[rows, cols] = (u[rows, cols] * mixed).astype(BF16)


def _in_proj(x2, mod, w_in_bf, cos, sin, ln_g, ln_b, ws_bf, bs_b, w_side, seq, q_width, kv_width,
             g_width, tm=512):
    n, d = x2.shape
    tiles_per_seq = seq // tm
    side_rows = _side_rows(w_side, n // tm)
    hdim = g_width // GMLP_HEADS
    const2 = lambda i: (0, 0)
    const3 = lambda i: (0, 0, 0)
    return pl.pallas_call(
        _in_proj_kernel,
        grid=(n // tm,),
        in_specs=[pl.BlockSpec((tm, d), lambda i: (i, 0)),
                  pl.BlockSpec((1, 1, d), lambda i: ((i // tiles_per_seq) * N_MOD + 0, 0, 0)),
                  pl.BlockSpec((1, 1, d), lambda i: ((i // tiles_per_seq) * N_MOD + 1, 0, 0)),
                  pl.BlockSpec(w_in_bf.shape, const2, pipeline_mode=pl.Buffered(1)),
                  pl.BlockSpec((tm, HEAD_DIM), lambda i: (i % tiles_per_seq, 0)),
                  pl.BlockSpec((tm, HEAD_DIM), lambda i: (i % tiles_per_seq, 0)),
                  pl.BlockSpec((1, g_width), const2),
                  pl.BlockSpec((1, g_width), const2),
                  pl.BlockSpec((GMLP_HEADS, CHUNK, CHUNK), const3),
                  pl.BlockSpec((GMLP_HEADS, CHUNK, hdim), const3),
                  pl.BlockSpec((side_rows, w_side.shape[1]), lambda i: (i, 0))],
        out_specs=[pl.BlockSpec((tm, q_width), lambda i: (i, 0)),
                   pl.BlockSpec((tm, kv_width), lambda i: (i, 0)),
                   pl.BlockSpec((tm, kv_width), lambda i: (i, 0)),
                   pl.BlockSpec((tm, g_width), lambda i: (i, 0)),
                   pl.BlockSpec((side_rows, w_side.shape[1]), lambda i: (i, 0))],
        out_shape=[jax.ShapeDtypeStruct((n, q_width), BF16),
                   jax.ShapeDtypeStruct((n, kv_width), BF16),
                   jax.ShapeDtypeStruct((n, kv_width), BF16),
                   jax.ShapeDtypeStruct((n, g_width), BF16),
                   jax.ShapeDtypeStruct(w_side.shape, BF16)],
        compiler_params=_params(1),
        name="in_proj",
    )(x2, mod, mod, w_in_bf, cos, sin, ln_g, ln_b, ws_bf, bs_b, w_side)


def _attn_kernel(sink_ref, q_ref, k_ref, v_ref, kc_ref, vc_ref, wside_ref, o_ref, wside_bf_ref):
    wside_bf_ref[...] = wside_ref[...].astype(BF16)
    tq = q_ref.shape[0]
    seq = k_ref.shape[0]
    n_loc = 3 * BLOCK
    rows = GROUP * BLOCK
    scale = HEAD_DIM ** -0.5
    t = pl.program_id(1)
    nt = (((1,), (1,)), ((), ()))

    row = lax.broadcasted_iota(jnp.int32, (rows, n_loc), 0)
    col = lax.broadcasted_iota(jnp.int32, (rows, n_loc), 1)
    rel = col - row % BLOCK
    head_of_row = lax.broadcasted_iota(jnp.int32, (rows, 1), 0) // BLOCK

    for hk in range(N_KV_HEADS):
        kv_cols = slice(hk * HEAD_DIM, (hk + 1) * HEAD_DIM)
        kc = kc_ref[:, kv_cols]
        vc = vc_ref[:, kv_cols]
        sink = jnp.zeros((rows, 1), F32)
        for g in range(GROUP):
            sink = jnp.where(head_of_row == g, sink_ref[hk * GROUP + g], sink)
        for jq in range(tq // BLOCK):
            blk = t * (tq // BLOCK) + jq
            ks = pl.multiple_of(jnp.clip((blk - 1) * BLOCK, 0, seq - n_loc), BLOCK)
            kl = k_ref[pl.ds(ks, n_loc), kv_cols]
            vl = v_ref[pl.ds(ks, n_loc), kv_cols]
            qs = jnp.concatenate(
                [q_ref[jq * BLOCK:(jq + 1) * BLOCK,
                       (hk * GROUP + g) * HEAD_DIM:(hk * GROUP + g + 1) * HEAD_DIM]
                 for g in range(GROUP)], axis=0)
            s_loc = lax.dot_general(qs, kl, nt, preferred_element_type=F32) * scale
            s_ctx = lax.dot_general(qs, kc, nt, preferred_element_type=F32) * scale
            band = jnp.abs(rel + (ks - blk * BLOCK)) <= WINDOW
            s_loc = jnp.where(band, s_loc, NEG_INF)
            m = jnp.maximum(jnp.maximum(jnp.max(s_loc, axis=-1, keepdims=True),
                                        jnp.max(s_ctx, axis=-1, keepdims=True)), sink)
            p_loc = jnp.exp(s_loc - m)
            p_ctx = jnp.exp(s_ctx - m)
            denom = (jnp.sum(p_loc, axis=-1, keepdims=True) + jnp.sum(p_ctx, axis=-1, keepdims=True)
                     + jnp.exp(sink - m))
            o = (jnp.dot(p_loc.astype(BF16), vl, preferred_element_type=F32)
                 + jnp.dot(p_ctx.astype(BF16), vc, preferred_element_type=F32)) / denom
            for g in range(GROUP):
                o_ref[jq * BLOCK:(jq + 1) * BLOCK,
                      (hk * GROUP + g) * HEAD_DIM:(hk * GROUP + g + 1) * HEAD_DIM] = (
                          o[g * BLOCK:(g + 1) * BLOCK].astype(BF16))


def _attention(sink, q, k, v, kc, vc, w_side, batch, seq, n_ctx, tq=512):
    q_width = q.shape[1]
    kv_width = k.shape[1]
    nt = seq // tq
    side_rows = _side_rows(w_side, batch * nt)
    return pl.pallas_call(
        _attn_kernel,
        grid=(batch, nt),
        in_specs=[pl.BlockSpec(memory_space=pltpu.SMEM),
                  pl.BlockSpec((tq, q_width), lambda b, t: (b * nt + t, 0)),
                  pl.BlockSpec((seq, kv_width), lambda b, t: (b, 0)),
                  pl.BlockSpec((seq, kv_width), lambda b, t: (b, 0)),
                  pl.BlockSpec((n_ctx, kv_width), lambda b, t: (b, 0)),
                  pl.BlockSpec((n_ctx, kv_width), lambda b, t: (b, 0)),
                  pl.BlockSpec((side_rows, w_side.shape[1]), lambda b, t: (b * nt + t, 0))],
        out_specs=[pl.BlockSpec((tq, q_width), lambda b, t: (b * nt + t, 0)),
                   pl.BlockSpec((side_rows, w_side.shape[1]), lambda b, t: (b * nt + t, 0))],
        out_shape=[jax.ShapeDtypeStruct(q.shape, BF16),
                   jax.ShapeDtypeStruct(w_side.shape, BF16)],
        compiler_params=_params(2),
        name="attention",
    )(sink, q, k, v, kc, vc, w_side)


def _out_proj_kernel(alpha, attn_ref, gm_ref, x_ref, gate_ref, shift_ref, scale_ref, w_ref,
                     g_ref, b_ref, wside_ref, x1_ref, h2_ref, wside_bf_ref):
    wside_bf_ref[...] = wside_ref[...].astype(BF16)
    aw = attn_ref.shape[1]
    mix = (jnp.dot(attn_ref[...], w_ref[:aw, :], preferred_element_type=F32)
           + jnp.dot(gm_ref[...], w_ref[aw:, :], preferred_element_type=F32))
    x1 = _layer_norm(alpha * x_ref[...] + gate_ref[0] * mix, g_ref[...], b_ref[...])
    x1_ref[...] = x1
    h2_ref[...] = (x1 * (1.0 + scale_ref[0]) + shift_ref[0]).astype(BF16)


def _out_proj(attn, gm, x2, mod, w_out_bf, ln_g, ln_b, w_side, seq, alpha, tm=512):
    n, d = x2.shape
    tiles_per_seq = seq // tm
    side_rows = _side_rows(w_side, n // tm)
    const2 = lambda i: (0, 0)
    mod_spec = lambda j: pl.BlockSpec((1, 1, d), lambda i: ((i // tiles_per_seq) * N_MOD + j, 0, 0))
    return pl.pallas_call(
        functools.partial(_out_proj_kernel, alpha),
        grid=(n // tm,),
        in_specs=[pl.BlockSpec((tm, attn.shape[1]), lambda i: (i, 0)),
                  pl.BlockSpec((tm, gm.shape[1]), lambda i: (i, 0)),
                  pl.BlockSpec((tm, d), lambda i: (i, 0)),
                  mod_spec(2), mod_spec(3), mod_spec(4),
                  pl.BlockSpec(w_out_bf.shape, const2, pipeline_mode=pl.Buffered(1)),
                  pl.BlockSpec((1, d), const2),
                  pl.BlockSpec((1, d), const2),
                  pl.BlockSpec((side_rows, w_side.shape[1]), lambda i: (i, 0))],
        out_specs=[pl.BlockSpec((tm, d), lambda i: (i, 0)),
                   pl.BlockSpec((tm, d), lambda i: (i, 0)),
                   pl.BlockSpec((side_rows, w_side.shape[1]), lambda i: (i, 0))],
        out_shape=[jax.ShapeDtypeStruct((n, d), F32),
                   jax.ShapeDtypeStruct((n, d), BF16),
                   jax.ShapeDtypeStruct(w_side.shape, BF16)],
        compiler_params=_params(1),
        name="out_proj",
    )(attn, gm, x2, mod, mod, mod, w_out_bf, ln_g, ln_b, w_side)


def _ffn_kernel(alpha, h_ref, x1_ref, gate_ref, w1_ref, w2_ref, g_ref, b_ref, o_ref, acc_ref):
    j = pl.program_id(1)

    @pl.when(j == 0)
    def _():
        acc_ref[...] = jnp.zeros_like(acc_ref)

    a = jnp.maximum(jnp.dot(h_ref[...], w1_ref[...], preferred_element_type=F32), 0.0)
    acc_ref[...] += jnp.dot((a * a).astype(BF16), w2_ref[...], preferred_element_type=F32)

    @pl.when(j == pl.num_programs(1) - 1)
    def _():
        o_ref[...] = _layer_norm(alpha * x1_ref[...] + gate_ref[0] * acc_ref[...],
                                 g_ref[...], b_ref[...])


def _ffn(h2, x1, mod, w1_bf, w2_bf, ln_g, ln_b, seq, alpha, tm=512, tf=1024):
    n, d = x1.shape
    d_ff = w1_bf.shape[1]
    tiles_per_seq = seq // tm
    const2 = lambda i, j: (0, 0)
    return pl.pallas_call(
        functools.partial(_ffn_kernel, alpha),
        grid=(n // tm, d_ff // tf),
        in_specs=[pl.BlockSpec((tm, d), lambda i, j: (i, 0)),
                  pl.BlockSpec((tm, d), lambda i, j: (i, 0)),
                  pl.BlockSpec((1, 1, d), lambda i, j: ((i // tiles_per_seq) * N_MOD + 5, 0, 0)),
                  pl.BlockSpec((d, tf), lambda i, j: (0, j)),
                  pl.BlockSpec((tf, d), lambda i, j: (j, 0)),
                  pl.BlockSpec((1, d), const2),
                  pl.BlockSpec((1, d), const2)],
        out_specs=pl.BlockSpec((tm, d), lambda i, j: (i, 0)),
        out_shape=jax.ShapeDtypeStruct((n, d), F32),
        scratch_shapes=[pltpu.VMEM((tm, d), F32)],
        compiler_params=_params(2),
        name="ffn",
    )(h2, x1, mod, w1_bf, w2_bf, ln_g, ln_b)


def _rope_tables(seq):
    rows = seq // GRID_W
    n_freq = HEAD_DIM // 4
    inv_freq = ROPE_BASE ** (-jnp.arange(n_freq, dtype=F32) / n_freq)
    ang_r = jnp.arange(rows, dtype=F32)[:, None] * inv_freq[None, :]
    ang_c = jnp.arange(GRID_W, dtype=F32)[:, None] * inv_freq[None, :]
    cos_r = jnp.repeat(jnp.cos(ang_r), GRID_W, axis=0)
    sin_r = jnp.repeat(jnp.sin(ang_r), GRID_W, axis=0)
    cos_c = jnp.tile(jnp.cos(ang_c), (rows, 1))
    sin_c = jnp.tile(jnp.sin(ang_c), (rows, 1))
    cos = jnp.concatenate([cos_r, cos_r, cos_c, cos_c], axis=-1)
    sin = jnp.concatenate([-sin_r, sin_r, -sin_c, sin_c], axis=-1)
    return cos, sin


def kernel(x, c, ctx, c_ctx, w_ada, b_ada, w_in, attn_sink, gmlp_ln_g, gmlp_ln_b, gmlp_w_s, gmlp_b_s,
           w_out, ln1_g, ln1_b, w_ff1, w_ff2, ln2_g, ln2_b):
    batch, seq, d = x.shape
    depth = w_ada.shape[0]
    assert depth == 1, "only the single-layer (last-layer) block is implemented"
    n_ctx = ctx.shape[1]
    q_width = N_Q_HEADS * HEAD_DIM
    kv_width = N_KV_HEADS * HEAD_DIM
    g_width = gmlp_ln_g.shape[1]
    assert w_in.shape[2] == q_width + 2 * kv_width + 2 * g_width
    alpha = (2 * depth) ** 0.25

    cond_rows = 8
    cond = jnp.zeros((cond_rows, d), F32).at[:batch].set(c).at[batch].set(c_ctx)
    mod = _adaln(cond, w_ada[0], b_ada[0][None, :]).reshape(cond_rows * N_MOD, 1, d)

    w_in_bf = w_in[0].astype(BF16)
    ws_bf = gmlp_w_s[0].astype(BF16)
    hdim = g_width // GMLP_HEADS
    bs_b = jnp.broadcast_to(gmlp_b_s[0][:, :, None], (GMLP_HEADS, CHUNK, hdim))

    x2 = x.reshape(batch * seq, d)
    kc, vc = _ctx_kv(ctx.reshape(batch * n_ctx, d), mod, w_in_bf, batch, n_ctx, q_width, kv_width)
    cos, sin = _rope_tables(seq)
    q, k, v, gm, w1_bf = _in_proj(x2, mod, w_in_bf, cos, sin, gmlp_ln_g[0][None, :],
                                  gmlp_ln_b[0][None, :], ws_bf, bs_b, w_ff1[0], seq, q_width,
                                  kv_width, g_width)
    attn, w_out_bf = _attention(attn_sink[0], q, k, v, kc, vc, w_out[0], batch, seq, n_ctx)
    x1, h2, w2_bf = _out_proj(attn, gm, x2, mod, w_out_bf, ln1_g[0][None, :], ln1_b[0][None, :],
                              w_ff2[0], seq, alpha)
    out = _ffn(h2, x1, mod, w1_bf, w2_bf, ln2_g[0][None, :], ln2_b[0][None, :], seq, alpha)
    return out.reshape(batch, seq, d)
```

```python
import functools

import jax
import jax.numpy as jnp
from jax import lax
from jax.experimental import pallas as pl
from jax.experimental.pallas import tpu as pltpu

BF16 = jnp.bfloat16
F32 = jnp.float32

GRID_W = 64
HEAD_DIM = 128
N_Q_HEADS = 8
N_KV_HEADS = 2
GROUP = N_Q_HEADS // N_KV_HEADS
GMLP_HEADS = 8
CHUNK = 128
WINDOW = 128
BLOCK = 128
N_MOD = 6
ROPE_BASE = 10000.0
LN_EPS = 1e-5
NEG_INF = -1e30

VMEM_LIMIT_BYTES = 56 * 1024 * 1024


def _params(n_axes):
    return pltpu.CompilerParams(
        dimension_semantics=("arbitrary",) * n_axes, vmem_limit_bytes=VMEM_LIMIT_BYTES)


def _side_rows(w, n_steps):
    rows = w.shape[0] // n_steps
    assert rows * n_steps == w.shape[0] and rows % 16 == 0
    return rows


def _layer_norm(y, g, b):
    mu = jnp.mean(y, axis=-1, keepdims=True)
    d = y - mu
    var = jnp.mean(d * d, axis=-1, keepdims=True)
    return d * lax.rsqrt(var + LN_EPS) * g + b


def _adaln_kernel(cond_ref, w_ref, b_ref, o_ref):
    c = cond_ref[...]
    s = (c * jax.nn.sigmoid(c)).astype(BF16)
    o_ref[...] = jnp.dot(s, w_ref[...].astype(BF16), preferred_element_type=F32) + b_ref[...]


def _adaln(cond, w_ada, b_ada, tn=1024):
    rows, d = cond.shape
    n = w_ada.shape[1]
    return pl.pallas_call(
        _adaln_kernel,
        grid=(n // tn,),
        in_specs=[pl.BlockSpec((rows, d), lambda j: (0, 0)),
                  pl.BlockSpec((d, tn), lambda j: (0, j)),
                  pl.BlockSpec((1, tn), lambda j: (0, j))],
        out_specs=pl.BlockSpec((rows, tn), lambda j: (0, j)),
        out_shape=jax.ShapeDtypeStruct((rows, n), F32),
        compiler_params=_params(1),
        name="adaln",
    )(cond, w_ada, b_ada)


def _ctx_kv_kernel(ctx_ref, shift_ref, scale_ref, w_ref, k_ref, v_ref):
    h = (ctx_ref[...] * (1.0 + scale_ref[0]) + shift_ref[0]).astype(BF16)
    kv = jnp.dot(h, w_ref[...], preferred_element_type=F32)
    kvw = k_ref.shape[1]
    k_ref[...] = kv[:, :kvw].astype(BF16)
    v_ref[...] = kv[:, kvw:].astype(BF16)


def _ctx_kv(ctx2, mod, w_in_bf, ctx_row, n_ctx, q_width, kv_width):
    rows, d = ctx2.shape
    assert q_width % (2 * kv_width) == 0
    out = jax.ShapeDtypeStruct((rows, kv_width), BF16)
    return pl.pallas_call(
        _ctx_kv_kernel,
        grid=(rows // n_ctx,),
        in_specs=[pl.BlockSpec((n_ctx, d), lambda i: (i, 0)),
                  pl.BlockSpec((1, 1, d), lambda i: (ctx_row * N_MOD + 0, 0, 0)),
                  pl.BlockSpec((1, 1, d), lambda i: (ctx_row * N_MOD + 1, 0, 0)),
                  pl.BlockSpec((d, 2 * kv_width), lambda i: (0, q_width // (2 * kv_width)))],
        out_specs=[pl.BlockSpec((n_ctx, kv_width), lambda i: (i, 0)),
                   pl.BlockSpec((n_ctx, kv_width), lambda i: (i, 0))],
        out_shape=[out, out],
        compiler_params=_params(1),
        name="ctx_kv",
    )(ctx2, mod, mod, w_in_bf)


def _in_proj_kernel(x_ref, shift_ref, scale_ref, w_ref, cosr_ref, sinr_ref, cosc_ref, sinc_ref,
                    lng_ref, lnb_ref, ws_ref, bs_ref, wside_ref,
                    q_ref, k_ref, v_ref, gm_ref, wside_bf_ref):
    wside_bf_ref[...] = wside_ref[...].astype(BF16)
    tm = x_ref.shape[0]
    qw = q_ref.shape[1]
    kvw = k_ref.shape[1]
    gw = gm_ref.shape[1]
    h = (x_ref[...] * (1.0 + scale_ref[0]) + shift_ref[0]).astype(BF16)

    def table(r_ref, c_ref):
        c = c_ref[...]
        return jnp.concatenate([r_ref[r:r + 1, :] + c for r in range(tm // GRID_W)], axis=0)

    cos = table(cosr_ref, cosc_ref)
    sin = table(sinr_ref, sinc_ref)
    lane = lax.broadcasted_iota(jnp.int32, (tm, HEAD_DIM), 1)
    low = (lane % (HEAD_DIM // 2)) < (HEAD_DIM // 4)

    def rope(t):
        partner = jnp.where(low, pltpu.roll(t, HEAD_DIM - HEAD_DIM // 4, 1),
                            pltpu.roll(t, HEAD_DIM // 4, 1))
        return t * cos + partner * sin

    def proj(lo, width):
        return jnp.dot(h, w_ref[:, lo:lo + width], preferred_element_type=F32)

    g = jax.nn.gelu(proj(qw + 2 * kvw + gw, gw))
    g = _layer_norm(g, lng_ref[...], lnb_ref[...]).astype(BF16)
    u = jax.nn.gelu(proj(qw + 2 * kvw, gw))

    qk = proj(0, qw + kvw)
    for hd in range((qw + kvw) // HEAD_DIM):
        t = rope(qk[:, hd * HEAD_DIM:(hd + 1) * HEAD_DIM]).astype(BF16)
        if hd * HEAD_DIM < qw:
            q_ref[:, hd * HEAD_DIM:(hd + 1) * HEAD_DIM] = t
        else:
            k_ref[:, hd * HEAD_DIM - qw:(hd + 1) * HEAD_DIM - qw] = t
    v_ref[...] = proj(qw + kvw, kvw).astype(BF16)

    hdim = gw // GMLP_HEADS
    for cb in range(tm // CHUNK):
        rows = slice(cb * CHUNK, (cb + 1) * CHUNK)
        for hd in range(GMLP_HEADS):
            cols = slice(hd * hdim, (hd + 1) * hdim)
            mixed = jnp.dot(ws_ref[hd], g[rows, cols], preferred_element_type=F32) + bs_ref[hd]
            gm_ref[rows, cols] = (u[rows, cols] * mixed).astype(BF16)


def _in_proj(x2, mod, w_in_bf, rope, ln_g, ln_b, ws_bf, bs_b, w_side, seq, q_width, kv_width,
             g_width, tm=512):
    n, d = x2.shape
    tiles_per_seq = seq // tm
    grid_rows = tm // GRID_W
    assert tm % GRID_W == 0 and grid_rows % 8 == 0
    row_spec = pl.BlockSpec((grid_rows, HEAD_DIM), lambda i: (i % tiles_per_seq, 0))
    col_spec = pl.BlockSpec((GRID_W, HEAD_DIM), lambda i: (0, 0))
    side_rows = _side_rows(w_side, n // tm)
    hdim = g_width // GMLP_HEADS
    const2 = lambda i: (0, 0)
    const3 = lambda i: (0, 0, 0)
    return pl.pallas_call(
        _in_proj_kernel,
        grid=(n // tm,),
        in_specs=[pl.BlockSpec((tm, d), lambda i: (i, 0)),
                  pl.BlockSpec((1, 1, d), lambda i: ((i // tiles_per_seq) * N_MOD + 0, 0, 0)),
                  pl.BlockSpec((1, 1, d), lambda i: ((i // tiles_per_seq) * N_MOD + 1, 0, 0)),
                  pl.BlockSpec(w_in_bf.shape, const2, pipeline_mode=pl.Buffered(1)),
                  row_spec, row_spec, col_spec, col_spec,
                  pl.BlockSpec((1, g_width), const2),
                  pl.BlockSpec((1, g_width), const2),
                  pl.BlockSpec((GMLP_HEADS, CHUNK, CHUNK), const3),
                  pl.BlockSpec((GMLP_HEADS, CHUNK, hdim), const3),
                  pl.BlockSpec((side_rows, w_side.shape[1]), lambda i: (i, 0))],
        out_specs=[pl.BlockSpec((tm, q_width), lambda i: (i, 0)),
                   pl.BlockSpec((tm, kv_width), lambda i: (i, 0)),
                   pl.BlockSpec((tm, kv_width), lambda i: (i, 0)),
                   pl.BlockSpec((tm, g_width), lambda i: (i, 0)),
                   pl.BlockSpec((side_rows, w_side.shape[1]), lambda i: (i, 0))],
        out_shape=[jax.ShapeDtypeStruct((n, q_width), BF16),
                   jax.ShapeDtypeStruct((n, kv_width), BF16),
                   jax.ShapeDtypeStruct((n, kv_width), BF16),
                   jax.ShapeDtypeStruct((n, g_width), BF16),
                   jax.ShapeDtypeStruct(w_side.shape, BF16)],
        compiler_params=_params(1),
        name="in_proj",
    )(x2, mod, mod, w_in_bf, *rope, ln_g, ln_b, ws_bf, bs_b, w_side)


def _attn_kernel(sink_ref, q_ref, k_ref, v_ref, kc_ref, vc_ref, wa_ref, wb_ref,
                 o_ref, wa_bf_ref, wb_bf_ref):
    wa_bf_ref[...] = wa_ref[...].astype(BF16)
    wb_bf_ref[...] = wb_ref[...].astype(BF16)
    tq = q_ref.shape[0]
    seq = k_ref.shape[0]
    n_blocks = seq // BLOCK
    rows = GROUP * BLOCK
    log2e = 1.4426950408889634
    qk_scale = HEAD_DIM ** -0.5 * log2e
    t = pl.program_id(1)
    nt = (((1,), (1,)), ((), ()))

    q_off = lax.broadcasted_iota(jnp.int32, (rows, BLOCK), 0) % BLOCK
    k_off = lax.broadcasted_iota(jnp.int32, (rows, BLOCK), 1)
    in_band_prev = k_off >= q_off
    in_band_next = k_off <= q_off
    head_of_row = lax.broadcasted_iota(jnp.int32, (rows, 1), 0) // BLOCK

    def scores(hk, jq):
        kv_cols = slice(hk * HEAD_DIM, (hk + 1) * HEAD_DIM)
        blk = t * (tq // BLOCK) + jq
        starts = [pl.multiple_of(jnp.maximum(blk - 1, 0) * BLOCK, BLOCK),
                  pl.multiple_of(blk * BLOCK, BLOCK),
                  pl.multiple_of(jnp.minimum(blk + 1, n_blocks - 1) * BLOCK, BLOCK)]
        kl = jnp.concatenate([k_ref[pl.ds(st, BLOCK), kv_cols] for st in starts], axis=0)
        qs = jnp.concatenate(
            [q_ref[jq * BLOCK:(jq + 1) * BLOCK,
                   (hk * GROUP + g) * HEAD_DIM:(hk * GROUP + g + 1) * HEAD_DIM]
             for g in range(GROUP)], axis=0)
        s_loc = lax.dot_general(qs, kl, nt, preferred_element_type=F32) * qk_scale
        s_ctx = lax.dot_general(qs, kc_ref[:, kv_cols], nt, preferred_element_type=F32) * qk_scale
        s_loc = jnp.concatenate(
            [jnp.where(in_band_prev & (blk > 0), s_loc[:, :BLOCK], NEG_INF),
             s_loc[:, BLOCK:2 * BLOCK],
             jnp.where(in_band_next & (blk < n_blocks - 1), s_loc[:, 2 * BLOCK:], NEG_INF)], axis=1)
        return s_loc, s_ctx, starts

    def finish(hk, jq, s_loc, s_ctx, starts):
        kv_cols = slice(hk * HEAD_DIM, (hk + 1) * HEAD_DIM)
        vl = jnp.concatenate([v_ref[pl.ds(st, BLOCK), kv_cols] for st in starts], axis=0)
        sink = jnp.zeros((rows, 1), F32)
        for g in range(GROUP):
            sink = jnp.where(head_of_row == g, sink_ref[hk * GROUP + g] * log2e, sink)
        m = jnp.maximum(jnp.maximum(jnp.max(s_loc, axis=-1, keepdims=True),
                                    jnp.max(s_ctx, axis=-1, keepdims=True)), sink)
        p_loc = jnp.exp2(s_loc - m)
        p_ctx = jnp.exp2(s_ctx - m)
        denom = (jnp.sum(p_loc, axis=-1, keepdims=True) + jnp.sum(p_ctx, axis=-1, keepdims=True)
                 + jnp.exp2(sink - m))
        o = (jnp.dot(p_loc.astype(BF16), vl, preferred_element_type=F32)
             + jnp.dot(p_ctx.astype(BF16), vc_ref[:, kv_cols], preferred_element_type=F32)) / denom
        for g in range(GROUP):
            o_ref[jq * BLOCK:(jq + 1) * BLOCK,
                  (hk * GROUP + g) * HEAD_DIM:(hk * GROUP + g + 1) * HEAD_DIM] = (
                      o[g * BLOCK:(g + 1) * BLOCK].astype(BF16))

    units = [(hk, jq) for hk in range(N_KV_HEADS) for jq in range(tq // BLOCK)]
    pending = scores(*units[0])
    for n, unit in enumerate(units):
        upcoming = scores(*units[n + 1]) if n + 1 < len(units) else None
        finish(*unit, *pending)
        pending = upcoming


def _attention(sink, q, k, v, kc, vc, w_side_a, w_side_b, batch, seq, n_ctx, tq=512):
    q_width = q.shape[1]
    kv_width = k.shape[1]
    nt = seq // tq
    side_specs = [pl.BlockSpec((_side_rows(w, batch * nt), w.shape[1]), lambda b, t: (b * nt + t, 0))
                  for w in (w_side_a, w_side_b)]
    return pl.pallas_call(
        _attn_kernel,
        grid=(batch, nt),
        in_specs=[pl.BlockSpec(memory_space=pltpu.SMEM),
                  pl.BlockSpec((tq, q_width), lambda b, t: (b * nt + t, 0)),
                  pl.BlockSpec((seq, kv_width), lambda b, t: (b, 0)),
                  pl.BlockSpec((seq, kv_width), lambda b, t: (b, 0)),
                  pl.BlockSpec((n_ctx, kv_width), lambda b, t: (b, 0)),
                  pl.BlockSpec((n_ctx, kv_width), lambda b, t: (b, 0))] + side_specs,
        out_specs=[pl.BlockSpec((tq, q_width), lambda b, t: (b * nt + t, 0))] + side_specs,
        out_shape=[jax.ShapeDtypeStruct(q.shape, BF16),
                   jax.ShapeDtypeStruct(w_side_a.shape, BF16),
                   jax.ShapeDtypeStruct(w_side_b.shape, BF16)],
        compiler_params=_params(2),
        name="attention",
    )(sink, q, k, v, kc, vc, w_side_a, w_side_b)


def _out_proj_kernel(alpha, attn_ref, gm_ref, x_ref, gate_ref, shift_ref, scale_ref, w_ref,
                     g_ref, b_ref, x1_ref, h2_ref):
    aw = attn_ref.shape[1]
    mix = (jnp.dot(attn_ref[...], w_ref[:aw, :], preferred_element_type=F32)
           + jnp.dot(gm_ref[...], w_ref[aw:, :], preferred_element_type=F32))
    x1 = _layer_norm(alpha * x_ref[...] + gate_ref[0] * mix, g_ref[...], b_ref[...])
    x1_ref[...] = x1
    h2_ref[...] = (x1 * (1.0 + scale_ref[0]) + shift_ref[0]).astype(BF16)


def _out_proj(attn, gm, x2, mod, w_out_bf, ln_g, ln_b, seq, alpha, tm=512):
    n, d = x2.shape
    tiles_per_seq = seq // tm
    const2 = lambda i: (0, 0)
    mod_spec = lambda j: pl.BlockSpec((1, 1, d), lambda i: ((i // tiles_per_seq) * N_MOD + j, 0, 0))
    return pl.pallas_call(
        functools.partial(_out_proj_kernel, alpha),
        grid=(n // tm,),
        in_specs=[pl.BlockSpec((tm, attn.shape[1]), lambda i: (i, 0)),
                  pl.BlockSpec((tm, gm.shape[1]), lambda i: (i, 0)),
                  pl.BlockSpec((tm, d), lambda i: (i, 0)),
                  mod_spec(2), mod_spec(3), mod_spec(4),
                  pl.BlockSpec(w_out_bf.shape, const2, pipeline_mode=pl.Buffered(1)),
                  pl.BlockSpec((1, d), const2),
                  pl.BlockSpec((1, d), const2)],
        out_specs=[pl.BlockSpec((tm, d), lambda i: (i, 0)),
                   pl.BlockSpec((tm, d), lambda i: (i, 0))],
        out_shape=[jax.ShapeDtypeStruct((n, d), F32),
                   jax.ShapeDtypeStruct((n, d), BF16)],
        compiler_params=_params(1),
        name="out_proj",
    )(attn, gm, x2, mod, mod, mod, w_out_bf, ln_g, ln_b)


def _ffn_kernel(alpha, h_ref, x1_ref, gate_ref, w1_ref, w2_ref, g_ref, b_ref, o_ref, acc_ref):
    j = pl.program_id(1)

    @pl.when(j == 0)
    def _():
        acc_ref[...] = jnp.zeros_like(acc_ref)

    a = jnp.maximum(jnp.dot(h_ref[...], w1_ref[...], preferred_element_type=F32), 0.0)
    acc_ref[...] += jnp.dot((a * a).astype(BF16), w2_ref[...], preferred_element_type=F32)

    @pl.when(j == pl.num_programs(1) - 1)
    def _():
        o_ref[...] = _layer_norm(alpha * x1_ref[...] + gate_ref[0] * acc_ref[...],
                                 g_ref[...], b_ref[...])


def _ffn(h2, x1, mod, w1_bf, w2_bf, ln_g, ln_b, seq, alpha, tm=512, tf=1024):
    n, d = x1.shape
    d_ff = w1_bf.shape[1]
    tiles_per_seq = seq // tm
    const2 = lambda i, j: (0, 0)
    return pl.pallas_call(
        functools.partial(_ffn_kernel, alpha),
        grid=(n // tm, d_ff // tf),
        in_specs=[pl.BlockSpec((tm, d), lambda i, j: (i, 0)),
                  pl.BlockSpec((tm, d), lambda i, j: (i, 0)),
                  pl.BlockSpec((1, 1, d), lambda i, j: ((i // tiles_per_seq) * N_MOD + 5, 0, 0)),
                  pl.BlockSpec((d, tf), lambda i, j: (0, j)),
                  pl.BlockSpec((tf, d), lambda i, j: (j, 0)),
                  pl.BlockSpec((1, d), const2),
                  pl.BlockSpec((1, d), const2)],
        out_specs=pl.BlockSpec((tm, d), lambda i, j: (i, 0)),
        out_shape=jax.ShapeDtypeStruct((n, d), F32),
        scratch_shapes=[pltpu.VMEM((tm, d), F32)],
        compiler_params=_params(2),
        name="ffn",
    )(h2, x1, mod, w1_bf, w2_bf, ln_g, ln_b)


def _rope_tables(seq):
    rows = seq // GRID_W
    n_freq = HEAD_DIM // 4
    inv_freq = ROPE_BASE ** (-jnp.arange(n_freq, dtype=F32) / n_freq)
    ang_r = jnp.arange(rows, dtype=F32)[:, None] * inv_freq[None, :]
    ang_c = jnp.arange(GRID_W, dtype=F32)[:, None] * inv_freq[None, :]
    zr = jnp.zeros((rows, HEAD_DIM // 2), F32)
    zc = jnp.zeros((GRID_W, HEAD_DIM // 2), F32)
    cos_r = jnp.concatenate([jnp.cos(ang_r), jnp.cos(ang_r), zr], axis=-1)
    sin_r = jnp.concatenate([-jnp.sin(ang_r), jnp.sin(ang_r), zr], axis=-1)
    cos_c = jnp.concatenate([zc, jnp.cos(ang_c), jnp.cos(ang_c)], axis=-1)
    sin_c = jnp.concatenate([zc, -jnp.sin(ang_c), jnp.sin(ang_c)], axis=-1)
    return cos_r, sin_r, cos_c, sin_c


def kernel(x, c, ctx, c_ctx, w_ada, b_ada, w_in, attn_sink, gmlp_ln_g, gmlp_ln_b, gmlp_w_s, gmlp_b_s,
           w_out, ln1_g, ln1_b, w_ff1, w_ff2, ln2_g, ln2_b):
    batch, seq, d = x.shape
    depth = w_ada.shape[0]
    assert depth == 1, "only the single-layer (last-layer) block is implemented"
    n_ctx = ctx.shape[1]
    q_width = N_Q_HEADS * HEAD_DIM
    kv_width = N_KV_HEADS * HEAD_DIM
    g_width = gmlp_ln_g.shape[1]
    assert w_in.shape[2] == q_width + 2 * kv_width + 2 * g_width
    alpha = (2 * depth) ** 0.25

    cond_rows = 8
    cond = jnp.zeros((cond_rows, d), F32).at[:batch].set(c).at[batch].set(c_ctx)
    mod = _adaln(cond, w_ada[0], b_ada[0][None, :]).reshape(cond_rows * N_MOD, 1, d)

    w_in_bf = w_in[0].astype(BF16)
    ws_bf = gmlp_w_s[0].astype(BF16)
    hdim = g_width // GMLP_HEADS
    bs_b = jnp.broadcast_to(gmlp_b_s[0][:, :, None], (GMLP_HEADS, CHUNK, hdim))

    x2 = x.reshape(batch * seq, d)
    kc, vc = _ctx_kv(ctx.reshape(batch * n_ctx, d), mod, w_in_bf, batch, n_ctx, q_width, kv_width)
    rope = _rope_tables(seq)
    q, k, v, gm, w1_bf = _in_proj(x2, mod, w_in_bf, rope, gmlp_ln_g[0][None, :],
                                  gmlp_ln_b[0][None, :], ws_bf, bs_b, w_ff1[0], seq, q_width,
                                  kv_width, g_width)
    attn, w_out_bf, w2_bf = _attention(attn_sink[0], q, k, v, kc, vc, w_out[0], w_ff2[0], batch, seq,
                                       n_ctx)
    x1, h2 = _out_proj(attn, gm, x2, mod, w_out_bf, ln1_g[0][None, :], ln1_b[0][None, :], seq, alpha)
    out = _ffn(h2, x1, mod, w1_bf, w2_bf, ln2_g[0][None, :], ln2_b[0][None, :], seq, alpha)
    return out.reshape(batch, seq, d)
```

```python
import functools

import jax
import jax.numpy as jnp
from jax import lax
from jax.experimental import pallas as pl
from jax.experimental.pallas import tpu as pltpu

BF16 = jnp.bfloat16
F32 = jnp.float32

GRID_W = 64
HEAD_DIM = 128
N_Q_HEADS = 8
N_KV_HEADS = 2
GROUP = N_Q_HEADS // N_KV_HEADS
GMLP_HEADS = 8
CHUNK = 128
WINDOW = 128
BLOCK = 128
N_MOD = 6
ROPE_BASE = 10000.0
LN_EPS = 1e-5
NEG_INF = -1e30

V7X_VMEM_BYTES = 64 * 1024 * 1024
VMEM_LIMIT_BYTES = V7X_VMEM_BYTES - 8 * 1024 * 1024
FFN_VMEM_LIMIT_BYTES = V7X_VMEM_BYTES - 2 * 1024 * 1024


def _params(n_axes, vmem_limit_bytes=VMEM_LIMIT_BYTES):
    return pltpu.CompilerParams(
        dimension_semantics=("arbitrary",) * n_axes, vmem_limit_bytes=vmem_limit_bytes)


def _side_rows(w, n_steps):
    rows = w.shape[0] // n_steps
    assert rows * n_steps == w.shape[0] and rows % 16 == 0
    return rows


def _layer_norm(y, g, b):
    mu = jnp.mean(y, axis=-1, keepdims=True)
    d = y - mu
    var = jnp.mean(d * d, axis=-1, keepdims=True)
    return d * lax.rsqrt(var + LN_EPS) * g + b


def _adaln_kernel(cond_ref, w_ref, b_ref, o_ref):
    c = cond_ref[...]
    s = (c * jax.nn.sigmoid(c)).astype(BF16)
    o_ref[...] = jnp.dot(s, w_ref[...].astype(BF16), preferred_element_type=F32) + b_ref[...]


def _adaln(cond, w_ada, b_ada, tn=1024):
    rows, d = cond.shape
    n = w_ada.shape[1]
    return pl.pallas_call(
        _adaln_kernel,
        grid=(n // tn,),
        in_specs=[pl.BlockSpec((rows, d), lambda j: (0, 0)),
                  pl.BlockSpec((d, tn), lambda j: (0, j)),
                  pl.BlockSpec((1, tn), lambda j: (0, j))],
        out_specs=pl.BlockSpec((rows, tn), lambda j: (0, j)),
        out_shape=jax.ShapeDtypeStruct((rows, n), F32),
        compiler_params=_params(1),
        name="adaln",
    )(cond, w_ada, b_ada)


def _ctx_kv_kernel(ctx_ref, shift_ref, scale_ref, w_ref, k_ref, v_ref):
    h = (ctx_ref[...] * (1.0 + scale_ref[0]) + shift_ref[0]).astype(BF16)
    kv = jnp.dot(h, w_ref[...], preferred_element_type=F32)
    kvw = k_ref.shape[1]
    k_ref[...] = kv[:, :kvw].astype(BF16)
    v_ref[...] = kv[:, kvw:].astype(BF16)


def _ctx_kv(ctx2, mod, w_in_bf, ctx_row, n_ctx, q_width, kv_width):
    rows, d = ctx2.shape
    assert q_width % (2 * kv_width) == 0
    out = jax.ShapeDtypeStruct((rows, kv_width), BF16)
    return pl.pallas_call(
        _ctx_kv_kernel,
        grid=(rows // n_ctx,),
        in_specs=[pl.BlockSpec((n_ctx, d), lambda i: (i, 0)),
                  pl.BlockSpec((1, 1, d), lambda i: (ctx_row * N_MOD + 0, 0, 0)),
                  pl.BlockSpec((1, 1, d), lambda i: (ctx_row * N_MOD + 1, 0, 0)),
                  pl.BlockSpec((d, 2 * kv_width), lambda i: (0, q_width // (2 * kv_width)))],
        out_specs=[pl.BlockSpec((n_ctx, kv_width), lambda i: (i, 0)),
                   pl.BlockSpec((n_ctx, kv_width), lambda i: (i, 0))],
        out_shape=[out, out],
        compiler_params=_params(1),
        name="ctx_kv",
    )(ctx2, mod, mod, w_in_bf)


def _in_proj_kernel(x_ref, shift_ref, scale_ref, w_ref, cosr_ref, sinr_ref, cosc_ref, sinc_ref,
                    lng_ref, lnb_ref, ws_ref, bs_ref, wside_ref,
                    q_ref, k_ref, v_ref, gm_ref, wside_bf_ref):
    wside_bf_ref[...] = wside_ref[...].astype(BF16)
    tm = x_ref.shape[0]
    qw = q_ref.shape[1]
    kvw = k_ref.shape[1]
    gw = gm_ref.shape[1]
    h = (x_ref[...] * (1.0 + scale_ref[0]) + shift_ref[0]).astype(BF16)

    def table(r_ref, c_ref):
        c = c_ref[...]
        return jnp.concatenate([r_ref[r:r + 1, :] + c for r in range(tm // GRID_W)], axis=0)

    cos = table(cosr_ref, cosc_ref)
    sin = table(sinr_ref, sinc_ref)
    lane = lax.broadcasted_iota(jnp.int32, (tm, HEAD_DIM), 1)
    low = (lane % (HEAD_DIM // 2)) < (HEAD_DIM // 4)

    def rope(t):
        partner = jnp.where(low, pltpu.roll(t, HEAD_DIM - HEAD_DIM // 4, 1),
                            pltpu.roll(t, HEAD_DIM // 4, 1))
        return t * cos + partner * sin

    def proj(lo, width):
        return jnp.dot(h, w_ref[:, lo:lo + width], preferred_element_type=F32)

    g = jax.nn.gelu(proj(qw + 2 * kvw + gw, gw))
    g = _layer_norm(g, lng_ref[...], lnb_ref[...]).astype(BF16)
    u = jax.nn.gelu(proj(qw + 2 * kvw, gw))

    qk = proj(0, qw + kvw)
    for hd in range((qw + kvw) // HEAD_DIM):
        t = rope(qk[:, hd * HEAD_DIM:(hd + 1) * HEAD_DIM]).astype(BF16)
        if hd * HEAD_DIM < qw:
            q_ref[:, hd * HEAD_DIM:(hd + 1) * HEAD_DIM] = t
        else:
            k_ref[:, hd * HEAD_DIM - qw:(hd + 1) * HEAD_DIM - qw] = t
    v_ref[...] = proj(qw + kvw, kvw).astype(BF16)

    hdim = gw // GMLP_HEADS
    for cb in range(tm // CHUNK):
        rows = slice(cb * CHUNK, (cb + 1) * CHUNK)
        for hd in range(GMLP_HEADS):
            cols = slice(hd * hdim, (hd + 1) * hdim)
            mixed = jnp.dot(ws_ref[hd], g[rows, cols], preferred_element_type=F32) + bs_ref[hd]
            gm_ref[rows, cols] = (u[rows, cols] * mixed).astype(BF16)


def _in_proj(x2, mod, w_in_bf, rope, ln_g, ln_b, ws_bf, bs_b, w_side, seq, q_width, kv_width,
             g_width, tm=512):
    n, d = x2.shape
    tiles_per_seq = seq // tm
    grid_rows = tm // GRID_W
    assert tm % GRID_W == 0 and grid_rows % 8 == 0
    row_spec = pl.BlockSpec((grid_rows, HEAD_DIM), lambda i: (i % tiles_per_seq, 0))
    col_spec = pl.BlockSpec((GRID_W, HEAD_DIM), lambda i: (0, 0))
    side_rows = _side_rows(w_side, n // tm)
    hdim = g_width // GMLP_HEADS
    const2 = lambda i: (0, 0)
    const3 = lambda i: (0, 0, 0)
    return pl.pallas_call(
        _in_proj_kernel,
        grid=(n // tm,),
        in_specs=[pl.BlockSpec((tm, d), lambda i: (i, 0)),
                  pl.BlockSpec((1, 1, d), lambda i: ((i // tiles_per_seq) * N_MOD + 0, 0, 0)),
                  pl.BlockSpec((1, 1, d), lambda i: ((i // tiles_per_seq) * N_MOD + 1, 0, 0)),
                  pl.BlockSpec(w_in_bf.shape, const2, pipeline_mode=pl.Buffered(1)),
                  row_spec, row_spec, col_spec, col_spec,
                  pl.BlockSpec((1, g_width), const2),
                  pl.BlockSpec((1, g_width), const2),
                  pl.BlockSpec((GMLP_HEADS, CHUNK, CHUNK), const3),
                  pl.BlockSpec((GMLP_HEADS, CHUNK, hdim), const3),
                  pl.BlockSpec((side_rows, w_side.shape[1]), lambda i: (i, 0))],
        out_specs=[pl.BlockSpec((tm, q_width), lambda i: (i, 0)),
                   pl.BlockSpec((tm, kv_width), lambda i: (i, 0)),
                   pl.BlockSpec((tm, kv_width), lambda i: (i, 0)),
                   pl.BlockSpec((tm, g_width), lambda i: (i, 0)),
                   pl.BlockSpec((side_rows, w_side.shape[1]), lambda i: (i, 0))],
        out_shape=[jax.ShapeDtypeStruct((n, q_width), BF16),
                   jax.ShapeDtypeStruct((n, kv_width), BF16),
                   jax.ShapeDtypeStruct((n, kv_width), BF16),
                   jax.ShapeDtypeStruct((n, g_width), BF16),
                   jax.ShapeDtypeStruct(w_side.shape, BF16)],
        compiler_params=_params(1),
        name="in_proj",
    )(x2, mod, mod, w_in_bf, *rope, ln_g, ln_b, ws_bf, bs_b, w_side)


def _attn_kernel(sink_ref, q_ref, k_ref, v_ref, kc_ref, vc_ref, wa_ref, wb_ref,
                 o_ref, wa_bf_ref, wb_bf_ref):
    wa_bf_ref[...] = wa_ref[...].astype(BF16)
    wb_bf_ref[...] = wb_ref[...].astype(BF16)
    tq = q_ref.shape[0]
    seq = k_ref.shape[0]
    n_blocks = seq // BLOCK
    rows = GROUP * BLOCK
    log2e = 1.4426950408889634
    qk_scale = HEAD_DIM ** -0.5 * log2e
    t = pl.program_id(1)
    nt = (((1,), (1,)), ((), ()))

    q_off = lax.broadcasted_iota(jnp.int32, (rows, BLOCK), 0) % BLOCK
    k_off = lax.broadcasted_iota(jnp.int32, (rows, BLOCK), 1)
    in_band_prev = k_off >= q_off
    in_band_next = k_off <= q_off
    head_of_row = lax.broadcasted_iota(jnp.int32, (rows, 1), 0) // BLOCK

    def scores(hk, jq):
        kv_cols = slice(hk * HEAD_DIM, (hk + 1) * HEAD_DIM)
        blk = t * (tq // BLOCK) + jq
        starts = [pl.multiple_of(jnp.maximum(blk - 1, 0) * BLOCK, BLOCK),
                  pl.multiple_of(blk * BLOCK, BLOCK),
                  pl.multiple_of(jnp.minimum(blk + 1, n_blocks - 1) * BLOCK, BLOCK)]
        kl = jnp.concatenate([k_ref[pl.ds(st, BLOCK), kv_cols] for st in starts], axis=0)
        qs = jnp.concatenate(
            [q_ref[jq * BLOCK:(jq + 1) * BLOCK,
                   (hk * GROUP + g) * HEAD_DIM:(hk * GROUP + g + 1) * HEAD_DIM]
             for g in range(GROUP)], axis=0)
        s_loc = lax.dot_general(qs, kl, nt, preferred_element_type=F32) * qk_scale
        s_ctx = lax.dot_general(qs, kc_ref[:, kv_cols], nt, preferred_element_type=F32) * qk_scale
        s_loc = jnp.concatenate(
            [jnp.where(in_band_prev & (blk > 0), s_loc[:, :BLOCK], NEG_INF),
             s_loc[:, BLOCK:2 * BLOCK],
             jnp.where(in_band_next & (blk < n_blocks - 1), s_loc[:, 2 * BLOCK:], NEG_INF)], axis=1)
        return s_loc, s_ctx, starts

    def finish(hk, jq, s_loc, s_ctx, starts):
        kv_cols = slice(hk * HEAD_DIM, (hk + 1) * HEAD_DIM)
        vl = jnp.concatenate([v_ref[pl.ds(st, BLOCK), kv_cols] for st in starts], axis=0)
        sink = jnp.zeros((rows, 1), F32)
        for g in range(GROUP):
            sink = jnp.where(head_of_row == g, sink_ref[hk * GROUP + g] * log2e, sink)
        m = jnp.maximum(jnp.maximum(jnp.max(s_loc, axis=-1, keepdims=True),
                                    jnp.max(s_ctx, axis=-1, keepdims=True)), sink)
        p_loc = jnp.exp2(s_loc - m)
        p_ctx = jnp.exp2(s_ctx - m)
        denom = (jnp.sum(p_loc, axis=-1, keepdims=True) + jnp.sum(p_ctx, axis=-1, keepdims=True)
                 + jnp.exp2(sink - m))
        o = (jnp.dot(p_loc.astype(BF16), vl, preferred_element_type=F32)
             + jnp.dot(p_ctx.astype(BF16), vc_ref[:, kv_cols], preferred_element_type=F32)) / denom
        for g in range(GROUP):
            o_ref[jq * BLOCK:(jq + 1) * BLOCK,
                  (hk * GROUP + g) * HEAD_DIM:(hk * GROUP + g + 1) * HEAD_DIM] = (
                      o[g * BLOCK:(g + 1) * BLOCK].astype(BF16))

    units = [(hk, jq) for hk in range(N_KV_HEADS) for jq in range(tq // BLOCK)]
    pending = scores(*units[0])
    for n, unit in enumerate(units):
        upcoming = scores(*units[n + 1]) if n + 1 < len(units) else None
        finish(*unit, *pending)
        pending = upcoming


def _attention(sink, q, k, v, kc, vc, w_side_a, w_side_b, batch, seq, n_ctx, tq=512):
    q_width = q.shape[1]
    kv_width = k.shape[1]
    nt = seq // tq
    side_specs = [pl.BlockSpec((_side_rows(w, batch * nt), w.shape[1]), lambda b, t: (b * nt + t, 0))
                  for w in (w_side_a, w_side_b)]
    return pl.pallas_call(
        _attn_kernel,
        grid=(batch, nt),
        in_specs=[pl.BlockSpec(memory_space=pltpu.SMEM),
                  pl.BlockSpec((tq, q_width), lambda b, t: (b * nt + t, 0)),
                  pl.BlockSpec((seq, kv_width), lambda b, t: (b, 0)),
                  pl.BlockSpec((seq, kv_width), lambda b, t: (b, 0)),
                  pl.BlockSpec((n_ctx, kv_width), lambda b, t: (b, 0)),
                  pl.BlockSpec((n_ctx, kv_width), lambda b, t: (b, 0))] + side_specs,
        out_specs=[pl.BlockSpec((tq, q_width), lambda b, t: (b * nt + t, 0))] + side_specs,
        out_shape=[jax.ShapeDtypeStruct(q.shape, BF16),
                   jax.ShapeDtypeStruct(w_side_a.shape, BF16),
                   jax.ShapeDtypeStruct(w_side_b.shape, BF16)],
        compiler_params=_params(2),
        name="attention",
    )(sink, q, k, v, kc, vc, w_side_a, w_side_b)


def _out_proj_kernel(alpha, attn_ref, gm_ref, x_ref, gate_ref, shift_ref, scale_ref, w_ref,
                     g_ref, b_ref, x1_ref, h2_ref):
    aw = attn_ref.shape[1]
    mix = (jnp.dot(attn_ref[...], w_ref[:aw, :], preferred_element_type=F32)
           + jnp.dot(gm_ref[...], w_ref[aw:, :], preferred_element_type=F32))
    x1 = _layer_norm(alpha * x_ref[...] + gate_ref[0] * mix, g_ref[...], b_ref[...])
    x1_ref[...] = x1
    h2_ref[...] = (x1 * (1.0 + scale_ref[0]) + shift_ref[0]).astype(BF16)


def _out_proj(attn, gm, x2, mod, w_out_bf, ln_g, ln_b, seq, alpha, tm=512):
    n, d = x2.shape
    tiles_per_seq = seq // tm
    const2 = lambda i: (0, 0)
    mod_spec = lambda j: pl.BlockSpec((1, 1, d), lambda i: ((i // tiles_per_seq) * N_MOD + j, 0, 0))
    return pl.pallas_call(
        functools.partial(_out_proj_kernel, alpha),
        grid=(n // tm,),
        in_specs=[pl.BlockSpec((tm, attn.shape[1]), lambda i: (i, 0)),
                  pl.BlockSpec((tm, gm.shape[1]), lambda i: (i, 0)),
                  pl.BlockSpec((tm, d), lambda i: (i, 0)),
                  mod_spec(2), mod_spec(3), mod_spec(4),
                  pl.BlockSpec(w_out_bf.shape, const2, pipeline_mode=pl.Buffered(1)),
                  pl.BlockSpec((1, d), const2),
                  pl.BlockSpec((1, d), const2)],
        out_specs=[pl.BlockSpec((tm, d), lambda i: (i, 0)),
                   pl.BlockSpec((tm, d), lambda i: (i, 0))],
        out_shape=[jax.ShapeDtypeStruct((n, d), F32),
                   jax.ShapeDtypeStruct((n, d), BF16)],
        compiler_params=_params(1),
        name="out_proj",
    )(attn, gm, x2, mod, mod, mod, w_out_bf, ln_g, ln_b)


def _ffn_kernel(alpha, h_ref, x1_ref, gate_ref, w1_hbm, w2_hbm, g_ref, b_ref, o_ref,
                w1_buf, w2_buf, sem):
    i = pl.program_id(0)
    n_slots, _, tf = w1_buf.shape
    n_chunks = w1_hbm.shape[1] // tf
    assert n_slots == 2 and n_chunks % n_slots == 0

    def fetch(c):
        slot = c % n_slots
        return (pltpu.make_async_copy(w1_hbm.at[:, pl.ds(c * tf, tf)], w1_buf.at[slot], sem.at[0, slot]),
                pltpu.make_async_copy(w2_hbm.at[pl.ds(c * tf, tf), :], w2_buf.at[slot], sem.at[1, slot]))

    @pl.when(i == 0)
    def _():
        for copy in fetch(0):
            copy.start()

    h = h_ref[...]
    acc = None
    for c in range(n_chunks):
        for copy in fetch(c):
            copy.wait()
        for copy in fetch((c + 1) % n_chunks):
            copy.start()
        slot = c % n_slots
        a = jnp.maximum(jnp.dot(h, w1_buf[slot], preferred_element_type=F32), 0.0)
        part = jnp.dot((a * a).astype(BF16), w2_buf[slot], preferred_element_type=F32)
        acc = part if acc is None else acc + part
    o_ref[...] = _layer_norm(alpha * x1_ref[...] + gate_ref[0] * acc, g_ref[...], b_ref[...])

    @pl.when(i == pl.num_programs(0) - 1)
    def _():
        for copy in fetch(0):
            copy.wait()


def _ffn(h2, x1, mod, w1_bf, w2_bf, ln_g, ln_b, seq, alpha, tm=512, tf=1024):
    n, d = x1.shape
    d_ff = w1_bf.shape[1]
    tiles_per_seq = seq // tm
    const2 = lambda i: (0, 0)
    return pl.pallas_call(
        functools.partial(_ffn_kernel, alpha),
        grid=(n // tm,),
        in_specs=[pl.BlockSpec((tm, d), lambda i: (i, 0)),
                  pl.BlockSpec((tm, d), lambda i: (i, 0)),
                  pl.BlockSpec((1, 1, d), lambda i: ((i // tiles_per_seq) * N_MOD + 5, 0, 0)),
                  pl.BlockSpec(memory_space=pl.ANY),
                  pl.BlockSpec(memory_space=pl.ANY),
                  pl.BlockSpec((1, d), const2),
                  pl.BlockSpec((1, d), const2)],
        out_specs=pl.BlockSpec((tm, d), lambda i: (i, 0)),
        out_shape=jax.ShapeDtypeStruct((n, d), F32),
        scratch_shapes=[pltpu.VMEM((2, d, tf), BF16),
                        pltpu.VMEM((2, tf, d), BF16),
                        pltpu.SemaphoreType.DMA((2, 2))],
        compiler_params=_params(1, FFN_VMEM_LIMIT_BYTES),
        name="ffn",
    )(h2, x1, mod, w1_bf, w2_bf, ln_g, ln_b)


def _rope_tables(seq):
    rows = seq // GRID_W
    n_freq = HEAD_DIM // 4
    inv_freq = ROPE_BASE ** (-jnp.arange(n_freq, dtype=F32) / n_freq)
    ang_r = jnp.arange(rows, dtype=F32)[:, None] * inv_freq[None, :]
    ang_c = jnp.arange(GRID_W, dtype=F32)[:, None] * inv_freq[None, :]
    zr = jnp.zeros((rows, HEAD_DIM // 2), F32)
    zc = jnp.zeros((GRID_W, HEAD_DIM // 2), F32)
    cos_r = jnp.concatenate([jnp.cos(ang_r), jnp.cos(ang_r), zr], axis=-1)
    sin_r = jnp.concatenate([-jnp.sin(ang_r), jnp.sin(ang_r), zr], axis=-1)
    cos_c = jnp.concatenate([zc, jnp.cos(ang_c), jnp.cos(ang_c)], axis=-1)
    sin_c = jnp.concatenate([zc, -jnp.sin(ang_c), jnp.sin(ang_c)], axis=-1)
    return cos_r, sin_r, cos_c, sin_c


def kernel(x, c, ctx, c_ctx, w_ada, b_ada, w_in, attn_sink, gmlp_ln_g, gmlp_ln_b, gmlp_w_s, gmlp_b_s,
           w_out, ln1_g, ln1_b, w_ff1, w_ff2, ln2_g, ln2_b):
    batch, seq, d = x.shape
    depth = w_ada.shape[0]
    assert depth == 1, "only the single-layer (last-layer) block is implemented"
    n_ctx = ctx.shape[1]
    q_width = N_Q_HEADS * HEAD_DIM
    kv_width = N_KV_HEADS * HEAD_DIM
    g_width = gmlp_ln_g.shape[1]
    assert w_in.shape[2] == q_width + 2 * kv_width + 2 * g_width
    alpha = (2 * depth) ** 0.25

    cond_rows = 8
    cond = jnp.zeros((cond_rows, d), F32).at[:batch].set(c).at[batch].set(c_ctx)
    mod = _adaln(cond, w_ada[0], b_ada[0][None, :]).reshape(cond_rows * N_MOD, 1, d)

    w_in_bf = w_in[0].astype(BF16)
    ws_bf = gmlp_w_s[0].astype(BF16)
    hdim = g_width // GMLP_HEADS
    bs_b = jnp.broadcast_to(gmlp_b_s[0][:, :, None], (GMLP_HEADS, CHUNK, hdim))

    x2 = x.reshape(batch * seq, d)
    kc, vc = _ctx_kv(ctx.reshape(batch * n_ctx, d), mod, w_in_bf, batch, n_ctx, q_width, kv_width)
    rope = _rope_tables(seq)
    q, k, v, gm, w1_bf = _in_proj(x2, mod, w_in_bf, rope, gmlp_ln_g[0][None, :],
                                  gmlp_ln_b[0][None, :], ws_bf, bs_b, w_ff1[0], seq, q_width,
                                  kv_width, g_width)
    attn, w_out_bf, w2_bf = _attention(attn_sink[0], q, k, v, kc, vc, w_out[0], w_ff2[0], batch, seq,
                                       n_ctx)
    x1, h2 = _out_proj(attn, gm, x2, mod, w_out_bf, ln1_g[0][None, :], ln1_b[0][None, :], seq, alpha)
    out = _ffn(h2, x1, mod, w1_bf, w2_bf, ln2_g[0][None, :], ln2_b[0][None, :], seq, alpha)
    return out.reshape(batch, seq, d)
```
